```python
import math
import jax, jax.numpy as jnp
from jax import lax
import numpy as np

D_MODEL = 2048
BATCH = 4
SEQ = 2048
DEPTH = 4

CONV_WIDTH = D_MODEL // 2
ATTN_WIDTH = D_MODEL - CONV_WIDTH
DIFF_HEAD_DIM = 64
V_HEAD_DIM = 2 * DIFF_HEAD_DIM
N_DIFF_HEADS = ATTN_WIDTH // V_HEAD_DIM
CONV_KERNEL = 31
D_FF = -(-8 * D_MODEL // (3 * 256)) * 256
IN_WIDTH = 2 * CONV_WIDTH + 3 * ATTN_WIDTH
Q_BLOCK = 128
EPS = 1e-6
LN_EPS = 1e-5

kernel_name = "hymba_conformer_conv_diff_attn_swiglu"


def rmsnorm(x, g):
    xf = x.astype(jnp.float32)
    y = xf * lax.rsqrt(jnp.mean(xf * xf, axis=-1, keepdims=True) + EPS)
    return (y * g.astype(jnp.float32)).astype(x.dtype)


def conformer_conv(u, w_dw, b_dw, ln_g, ln_b):
    a, gate = jnp.split(u, 2, axis=-1)
    h = a * jax.nn.sigmoid(gate)
    h = jnp.pad(h, ((0, 0), (CONV_KERNEL - 1, 0), (0, 0)))
    h = lax.conv_general_dilated(
        h, w_dw[:, None, :], window_strides=(1,), padding='VALID',
        dimension_numbers=('NWC', 'WIO', 'NWC'),
        feature_group_count=CONV_WIDTH) + b_dw
    hf = h.astype(jnp.float32)
    mu = jnp.mean(hf, axis=-1, keepdims=True)
    var = jnp.mean(jnp.square(hf - mu), axis=-1, keepdims=True)
    hf = (hf - mu) * lax.rsqrt(var + LN_EPS) * ln_g.astype(jnp.float32) + ln_b.astype(jnp.float32)
    return jax.nn.silu(hf).astype(u.dtype)


def diff_attention(q, k, v, lam):
    seq = q.shape[1]
    scale = DIFF_HEAD_DIM ** -0.5
    outs = []
    for i in range(seq // Q_BLOCK):
        q0 = i * Q_BLOCK
        kend = q0 + Q_BLOCK
        qb = q[:, q0:kend]
        kb = k[:, :kend]
        vb = v[:, :kend]
        s = jnp.einsum('bqhcd,bkhcd->bhcqk', qb, kb).astype(jnp.float32) * scale
        qpos = q0 + jnp.arange(Q_BLOCK)[:, None]
        kpos = jnp.arange(kend)[None, :]
        s = jnp.where(kpos <= qpos, s, -jnp.inf)
        p = jax.nn.softmax(s, axis=-1)
        a = p[:, :, 0] - lam * p[:, :, 1]
        outs.append(jnp.einsum('bhqk,bkhd->bqhd', a.astype(v.dtype), vb))
    return jnp.concatenate(outs, axis=1)


def setup_inputs(seed: int = 0) -> dict:
    key = jax.random.key(seed)
    ks = jax.random.split(key, 20)
    f32 = jnp.float32
    nrm = lambda k, shape, s: jax.random.normal(k, shape, f32) * s
    return {
        "x": nrm(ks[0], (BATCH, SEQ, D_MODEL), 1.0),
        "norm1_g": 1.0 + nrm(ks[1], (DEPTH, D_MODEL), 0.02),
        "w_in": nrm(ks[2], (DEPTH, D_MODEL, IN_WIDTH), D_MODEL ** -0.5),
        "conv_w": nrm(ks[3], (DEPTH, CONV_KERNEL, CONV_WIDTH), CONV_KERNEL ** -0.5),
        "conv_b": nrm(ks[4], (DEPTH, CONV_WIDTH), 0.02),
        "conv_ln_g": 1.0 + nrm(ks[5], (DEPTH, CONV_WIDTH), 0.02),
        "conv_ln_b": nrm(ks[6], (DEPTH, CONV_WIDTH), 0.02),
        "lam_q1": nrm(ks[7], (DEPTH, DIFF_HEAD_DIM), 0.1),
        "lam_k1": nrm(ks[8], (DEPTH, DIFF_HEAD_DIM), 0.1),
        "lam_q2": nrm(ks[9], (DEPTH, DIFF_HEAD_DIM), 0.1),
        "lam_k2": nrm(ks[10], (DEPTH, DIFF_HEAD_DIM), 0.1),
        "subln_g": 1.0 + nrm(ks[11], (DEPTH, V_HEAD_DIM), 0.02),
        "w_out": nrm(ks[12], (DEPTH, D_MODEL, D_MODEL), D_MODEL ** -0.5),
        "norm2_g": 1.0 + nrm(ks[13], (DEPTH, D_MODEL), 0.02),
        "w_gate": nrm(ks[14], (DEPTH, D_MODEL, D_FF), D_MODEL ** -0.5),
        "w_up": nrm(ks[15], (DEPTH, D_MODEL, D_FF), D_MODEL ** -0.5),
        "w_down": nrm(ks[16], (DEPTH, D_FF, D_MODEL), D_FF ** -0.5),
        "final_g": 1.0 + nrm(ks[17], (D_MODEL,), 0.02),
    }


def reference(x, norm1_g, w_in, conv_w, conv_b, conv_ln_g, conv_ln_b,
              lam_q1, lam_k1, lam_q2, lam_k2, subln_g, w_out, norm2_g,
              w_gate, w_up, w_down, final_g):
    B, S, _ = x.shape
    c2 = 2 * CONV_WIDTH
    for l in range(DEPTH):
        lam_init = 0.8 - 0.6 * math.exp(-0.3 * l)
        h = rmsnorm(x, norm1_g[l])
        z = h @ w_in[l]
        u_conv = z[..., :c2]
        q = z[..., c2:c2 + ATTN_WIDTH].reshape(B, S, N_DIFF_HEADS, 2, DIFF_HEAD_DIM)
        k = z[..., c2 + ATTN_WIDTH:c2 + 2 * ATTN_WIDTH].reshape(B, S, N_DIFF_HEADS, 2, DIFF_HEAD_DIM)
        v = z[..., c2 + 2 * ATTN_WIDTH:].reshape(B, S, N_DIFF_HEADS, V_HEAD_DIM)

        conv_out = conformer_conv(u_conv, conv_w[l], conv_b[l], conv_ln_g[l], conv_ln_b[l])

        lam = (jnp.exp(jnp.sum(lam_q1[l].astype(jnp.float32) * lam_k1[l].astype(jnp.float32)))
               - jnp.exp(jnp.sum(lam_q2[l].astype(jnp.float32) * lam_k2[l].astype(jnp.float32)))
               + lam_init)
        o = diff_attention(q, k, v, lam)
        o = rmsnorm(o, subln_g[l]) * (1.0 - lam_init)
        attn_out = o.reshape(B, S, ATTN_WIDTH)

        mix = jnp.concatenate([conv_out, attn_out], axis=-1) @ w_out[l]
        x = x + mix
        h = rmsnorm(x, norm2_g[l])
        x = x + (jax.nn.silu(h @ w_gate[l]) * (h @ w_up[l])) @ w_down[l]
    return rmsnorm(x, final_g)
```

```python
import functools
import math

import jax
import jax.numpy as jnp
from jax import lax
from jax.experimental import pallas as pl
from jax.experimental.pallas import tpu as pltpu

D_MODEL = 2048
CONV_WIDTH = 1024
ATTN_WIDTH = 1024
DIFF_HEAD_DIM = 64
V_HEAD_DIM = 128
N_DIFF_HEADS = 8
CONV_KERNEL = 31
EPS = 1e-6
LN_EPS = 1e-5

LANES = 128
CONV_HALO = 32
VMEM_LIMIT = 48 * 1024 * 1024

BF16 = jnp.bfloat16
F32 = jnp.float32


def _rms_rows(x, g):
    ms = jnp.mean(x * x, axis=-1, keepdims=True)
    return x * lax.rsqrt(ms + EPS) * g


def _norm_matmul_kernel(x_ref, g_ref, w_ref, o_ref, h_ref):
    @pl.when(pl.program_id(1) == 0)
    def _():
        h_ref[...] = _rms_rows(x_ref[...], g_ref[...]).astype(BF16)

    o_ref[...] = jnp.dot(h_ref[...], w_ref[...], preferred_element_type=F32).astype(o_ref.dtype)


def _norm_matmul(x, g, w, *, tm, tn):
    t, d = x.shape
    n = w.shape[1]
    return pl.pallas_call(
        _norm_matmul_kernel,
        grid=(t // tm, n // tn),
        in_specs=[
            pl.BlockSpec((tm, d), lambda i, j: (i, 0)),
            pl.BlockSpec((1, d), lambda i, j: (0, 0)),
            pl.BlockSpec((d, tn), lambda i, j: (0, j)),
        ],
        out_specs=pl.BlockSpec((tm, tn), lambda i, j: (i, j)),
        out_shape=jax.ShapeDtypeStruct((t, n), BF16),
        scratch_shapes=[pltpu.VMEM((tm, d), BF16)],
        compiler_params=pltpu.CompilerParams(
            dimension_semantics=("arbitrary", "arbitrary"), vmem_limit_bytes=VMEM_LIMIT),
        name="norm_in_proj",
    )(x, g, w)


def _conv_kernel(a_ref, gate_ref, w_ref, b_ref, lng_ref, lnb_ref, o_ref, hbuf, ybuf, *, ts, rows):
    n_ct = CONV_WIDTH // LANES

    @pl.when(pl.program_id(1) == 0)
    def _():
        hbuf[:, 0:CONV_HALO, :] = jnp.zeros((n_ct, CONV_HALO, LANES), F32)

    @pl.when(pl.program_id(1) != 0)
    def _():
        hbuf[:, 0:CONV_HALO, :] = hbuf[:, ts:ts + CONV_HALO, :]

    for c in range(n_ct):
        cs = slice(c * LANES, (c + 1) * LANES)
        a = a_ref[:, cs].astype(F32)
        gt = gate_ref[:, cs].astype(F32)
        hbuf[c, CONV_HALO:CONV_HALO + ts, :] = a * jax.nn.sigmoid(gt)

    base = CONV_HALO - (CONV_KERNEL - 1)
    for c in range(n_ct):
        cs = slice(c * LANES, (c + 1) * LANES)
        for r in range(ts // rows):
            acc = jnp.zeros((rows, LANES), F32) + b_ref[:, cs]
            for j in range(CONV_KERNEL):
                acc = acc + w_ref[j:j + 1, cs] * hbuf[c, pl.ds(r * rows + base + j, rows), :]
            ybuf[r * rows:(r + 1) * rows, cs] = acc

    y = ybuf[...]
    mu = jnp.mean(y, axis=-1, keepdims=True)
    yc = y - mu
    var = jnp.mean(yc * yc, axis=-1, keepdims=True)
    yn = yc * lax.rsqrt(var + LN_EPS) * lng_ref[...] + lnb_ref[...]
    o_ref[...] = (yn * jax.nn.sigmoid(yn)).astype(o_ref.dtype)


def _conv_branch(z, w, b, lng, lnb, *, batch, seq, ts, rows):
    t = z.shape[0]
    nt = seq // ts
    n_ct = CONV_WIDTH // LANES
    vec = pl.BlockSpec((1, CONV_WIDTH), lambda bi, si: (0, 0))
    return pl.pallas_call(
        functools.partial(_conv_kernel, ts=ts, rows=rows),
        grid=(batch, nt),
        in_specs=[
            pl.BlockSpec((ts, CONV_WIDTH), lambda bi, si: (bi * nt + si, 0)),
            pl.BlockSpec((ts, CONV_WIDTH), lambda bi, si: (bi * nt + si, 1)),
            pl.BlockSpec((CONV_KERNEL, CONV_WIDTH), lambda bi, si: (0, 0)),
            vec, vec, vec,
        ],
        out_specs=pl.BlockSpec((ts, CONV_WIDTH), lambda bi, si: (bi * nt + si, 0)),
        out_shape=jax.ShapeDtypeStruct((t, CONV_WIDTH), BF16),
        scratch_shapes=[
            pltpu.VMEM((n_ct, ts + CONV_HALO, LANES), F32),
            pltpu.VMEM((ts, CONV_WIDTH), F32),
        ],
        compiler_params=pltpu.CompilerParams(
            dimension_semantics=("arbitrary", "arbitrary"), vmem_limit_bytes=VMEM_LIMIT),
        name="conv_branch",
    )(z, z, w, b, lng, lnb)


def _attn_kernel(q_ref, k_ref, v_ref, lq1_ref, lk1_ref, lq2_ref, lk2_ref, sg_ref, o_ref,
                 *, tq, lam_init):
    qi = pl.program_id(2)
    lam = (jnp.exp(jnp.sum(lq1_ref[...] * lk1_ref[...], axis=-1, keepdims=True))
           - jnp.exp(jnp.sum(lq2_ref[...] * lk2_ref[...], axis=-1, keepdims=True))
           + lam_init)

    lane = lax.broadcasted_iota(jnp.int32, (tq, V_HEAD_DIM), 1)
    q = q_ref[...] * (DIFF_HEAD_DIM ** -0.5)
    zero = jnp.zeros_like(q)
    qs = (jnp.where(lane < DIFF_HEAD_DIM, q, zero), jnp.where(lane >= DIFF_HEAD_DIM, q, zero))

    def scores(qm, kb):
        return lax.dot_general(qm, kb, (((1,), (1,)), ((), ())), preferred_element_type=F32)

    def update(carry, s, vb):
        m, l, acc = carry
        m_new = jnp.maximum(m, jnp.max(s, axis=-1, keepdims=True))
        alpha = jnp.exp(m - m_new)
        p = jnp.exp(s - m_new)
        l_new = alpha * l + jnp.sum(p, axis=-1, keepdims=True)
        acc_new = alpha * acc + jnp.dot(p.astype(BF16), vb, preferred_element_type=F32)
        return m_new, l_new, acc_new

    def init():
        return (jnp.full((tq, 1), -jnp.inf, F32), jnp.zeros((tq, 1), F32),
                jnp.zeros((tq, V_HEAD_DIM), F32))

    def body(ki, carry):
        off = pl.multiple_of(ki * tq, tq)
        kb = k_ref[pl.ds(off, tq), :]
        vb = v_ref[pl.ds(off, tq), :]
        return tuple(update(c, scores(qm, kb), vb) for c, qm in zip(carry, qs))

    carry = lax.fori_loop(0, qi, body, (init(), init()))

    off = pl.multiple_of(qi * tq, tq)
    kb = k_ref[pl.ds(off, tq), :]
    vb = v_ref[pl.ds(off, tq), :]
    row = lax.broadcasted_iota(jnp.int32, (tq, tq), 0)
    col = lax.broadcasted_iota(jnp.int32, (tq, tq), 1)
    causal = col <= row
    outs = []
    for c, qm in zip(carry, qs):
        s = jnp.where(causal, scores(qm, kb), -jnp.inf)
        _, l, acc = update(c, s, vb)
        outs.append(acc / l)

    o = outs[0] - lam * outs[1]
    o = _rms_rows(o, sg_ref[...]) * (1.0 - lam_init)
    o_ref[...] = o.astype(o_ref.dtype)


def _diff_attention(z, lq1, lk1, lq2, lk2, sg, *, batch, seq, tq, lam_init):
    t = z.shape[0]
    nq = seq // tq
    q0 = 2 * CONV_WIDTH // V_HEAD_DIM
    k0 = q0 + N_DIFF_HEADS
    v0 = k0 + N_DIFF_HEADS
    lam_spec = pl.BlockSpec((1, DIFF_HEAD_DIM), lambda b, h, i: (0, 0))
    return pl.pallas_call(
        functools.partial(_attn_kernel, tq=tq, lam_init=lam_init),
        grid=(batch, N_DIFF_HEADS, nq),
        in_specs=[
            pl.BlockSpec((tq, V_HEAD_DIM), lambda b, h, i: (b * nq + i, q0 + h)),
            pl.BlockSpec((seq, V_HEAD_DIM), lambda b, h, i: (b, k0 + h)),
            pl.BlockSpec((seq, V_HEAD_DIM), lambda b, h, i: (b, v0 + h)),
            lam_spec, lam_spec, lam_spec, lam_spec,
            pl.BlockSpec((1, V_HEAD_DIM), lambda b, h, i: (0, 0)),
        ],
        out_specs=pl.BlockSpec((tq, V_HEAD_DIM), lambda b, h, i: (b * nq + i, h)),
        out_shape=jax.ShapeDtypeStruct((t, ATTN_WIDTH), BF16),
        compiler_params=pltpu.CompilerParams(
            dimension_semantics=("arbitrary", "arbitrary", "arbitrary"),
            vmem_limit_bytes=VMEM_LIMIT),
        name="diff_attention",
    )(z, z, z, lq1, lk1, lq2, lk2, sg)


def _out_proj_kernel(c_ref, a_ref, w_ref, x_ref, o_ref):
    acc = jnp.dot(c_ref[...], w_ref[0:CONV_WIDTH, :], preferred_element_type=F32)
    acc = acc + jnp.dot(a_ref[...], w_ref[CONV_WIDTH:, :], preferred_element_type=F32)
    o_ref[...] = x_ref[...] + acc


def _out_proj(conv_out, attn_out, w, x, *, tm, tn):
    t, d = x.shape
    return pl.pallas_call(
        _out_proj_kernel,
        grid=(t // tm, d // tn),
        in_specs=[
            pl.BlockSpec((tm, CONV_WIDTH), lambda i, j: (i, 0)),
            pl.BlockSpec((tm, ATTN_WIDTH), lambda i, j: (i, 0)),
            pl.BlockSpec((d, tn), lambda i, j: (0, j)),
            pl.BlockSpec((tm, tn), lambda i, j: (i, j)),
        ],
        out_specs=pl.BlockSpec((tm, tn), lambda i, j: (i, j)),
        out_shape=jax.ShapeDtypeStruct((t, d), F32),
        compiler_params=pltpu.CompilerParams(
            dimension_semantics=("arbitrary", "arbitrary"), vmem_limit_bytes=VMEM_LIMIT),
        name="out_proj",
    )(conv_out, attn_out, w, x)


def _ffn_kernel(x_ref, g_ref, wg_ref, wu_ref, wd_ref, o_ref, h_ref):
    @pl.when(pl.program_id(1) == 0)
    def _():
        x = x_ref[...]
        h_ref[...] = _rms_rows(x, g_ref[...]).astype(BF16)
        o_ref[...] = x

    h = h_ref[...]
    gate = jnp.dot(h, wg_ref[...], preferred_element_type=F32)
    up = jnp.dot(h, wu_ref[...], preferred_element_type=F32)
    act = (gate * jax.nn.sigmoid(gate) * up).astype(BF16)
    o_ref[...] += jnp.dot(act, wd_ref[...], preferred_element_type=F32)


def _ffn(x, g, wg, wu, wd, *, tm, tf):
    t, d = x.shape
    f = wg.shape[1]
    return pl.pallas_call(
        _ffn_kernel,
        grid=(t // tm, f // tf),
        in_specs=[
            pl.BlockSpec((tm, d), lambda i, j: (i, 0)),
            pl.BlockSpec((1, d), lambda i, j: (0, 0)),
            pl.BlockSpec((d, tf), lambda i, j: (0, j)),
            pl.BlockSpec((d, tf), lambda i, j: (0, j)),
            pl.BlockSpec((tf, d), lambda i, j: (j, 0)),
        ],
        out_specs=pl.BlockSpec((tm, d), lambda i, j: (i, 0)),
        out_shape=jax.ShapeDtypeStruct((t, d), F32),
        scratch_shapes=[pltpu.VMEM((tm, d), BF16)],
        compiler_params=pltpu.CompilerParams(
            dimension_semantics=("arbitrary", "arbitrary"), vmem_limit_bytes=VMEM_LIMIT),
        name="ffn",
    )(x, g, wg, wu, wd)


def _final_norm_kernel(x_ref, g_ref, o_ref):
    o_ref[...] = _rms_rows(x_ref[...], g_ref[...])


def _final_norm(x, g, *, tm):
    t, d = x.shape
    return pl.pallas_call(
        _final_norm_kernel,
        grid=(t // tm,),
        in_specs=[pl.BlockSpec((tm, d), lambda i: (i, 0)), pl.BlockSpec((1, d), lambda i: (0, 0))],
        out_specs=pl.BlockSpec((tm, d), lambda i: (i, 0)),
        out_shape=jax.ShapeDtypeStruct((t, d), F32),
        compiler_params=pltpu.CompilerParams(
            dimension_semantics=("arbitrary",), vmem_limit_bytes=VMEM_LIMIT),
        name="final_norm",
    )(x, g)


def kernel(x, norm1_g, w_in, conv_w, conv_b, conv_ln_g, conv_ln_b, lam_q1, lam_k1, lam_q2, lam_k2,
           subln_g, w_out, norm2_g, w_gate, w_up, w_down, final_g):
    batch, seq, d = x.shape
    depth = w_in.shape[0]
    xt = x.reshape(batch * seq, d)
    row = lambda v: v.reshape(1, -1)
    for l in range(depth):
        lam_init = 0.8 - 0.6 * math.exp(-0.3 * l)
        z = _norm_matmul(xt, row(norm1_g[l]), w_in[l].astype(BF16), tm=512, tn=512)
        conv_out = _conv_branch(z, conv_w[l], row(conv_b[l]), row(conv_ln_g[l]), row(conv_ln_b[l]),
                                batch=batch, seq=seq, ts=256, rows=64)
        attn_out = _diff_attention(z, row(lam_q1[l]), row(lam_k1[l]), row(lam_q2[l]), row(lam_k2[l]),
                                   row(subln_g[l]), batch=batch, seq=seq, tq=256, lam_init=lam_init)
        xt = _out_proj(conv_out, attn_out, w_out[l].astype(BF16), xt, tm=512, tn=512)
        xt = _ffn(xt, row(norm2_g[l]), w_gate[l].astype(BF16), w_up[l].astype(BF16),
                  w_down[l].astype(BF16), tm=512, tf=512)
    out = _final_norm(xt, row(final_g), tm=512)
    return out.reshape(batch, seq, d)
```

```python
import functools
import math

import jax
import jax.numpy as jnp
from jax import lax
from jax.experimental import pallas as pl
from jax.experimental.pallas import tpu as pltpu

D_MODEL = 2048
CONV_WIDTH = 1024
ATTN_WIDTH = 1024
DIFF_HEAD_DIM = 64
V_HEAD_DIM = 128
N_DIFF_HEADS = 8
CONV_KERNEL = 31
EPS = 1e-6
LN_EPS = 1e-5

LANES = 128
CONV_HALO = 32
VMEM_LIMIT = 48 * 1024 * 1024

BF16 = jnp.bfloat16
F32 = jnp.float32

IN_PROJ_TM, IN_PROJ_TN = 1024, 1024
CONV_TS, CONV_ROWS = 256, 64
ATTN_TQ = 256
OUT_PROJ_TM = 512
FFN_TM, FFN_TF = 512, 512
FINAL_TM = 512


def _rms_rows(x, g):
    ms = jnp.mean(x * x, axis=-1, keepdims=True)
    return x * lax.rsqrt(ms + EPS) * g


def _layer_vec(l, width, n_grid):
    if n_grid == 1:
        return pl.BlockSpec((None, 1, width), lambda i: (l, 0, 0))
    return pl.BlockSpec((None, 1, width), lambda i, j: (l, 0, 0))


def _params(n_grid):
    return pltpu.CompilerParams(
        dimension_semantics=("arbitrary",) * n_grid, vmem_limit_bytes=VMEM_LIMIT)


def _norm_matmul_kernel(x_ref, g_ref, w_ref, o_ref, h_ref):
    @pl.when(pl.program_id(1) == 0)
    def _():
        h_ref[...] = _rms_rows(x_ref[...], g_ref[...]).astype(BF16)

    o_ref[...] = jnp.dot(h_ref[...], w_ref[...], preferred_element_type=F32).astype(o_ref.dtype)


def _norm_matmul(x, g, w, l):
    t, d = x.shape
    n = w.shape[2]
    tm, tn = IN_PROJ_TM, IN_PROJ_TN
    return pl.pallas_call(
        _norm_matmul_kernel,
        grid=(t // tm, n // tn),
        in_specs=[
            pl.BlockSpec((tm, d), lambda i, j: (i, 0)),
            _layer_vec(l, d, 2),
            pl.BlockSpec((None, d, tn), lambda i, j: (l, 0, j)),
        ],
        out_specs=pl.BlockSpec((tm, tn), lambda i, j: (i, j)),
        out_shape=jax.ShapeDtypeStruct((t, n), BF16),
        scratch_shapes=[pltpu.VMEM((tm, d), BF16)],
        compiler_params=_params(2),
        name="norm_in_proj",
    )(x, g, w)


def _conv_kernel(a_ref, gate_ref, w_ref, b_ref, lng_ref, lnb_ref, o_ref, hbuf, ybuf, *, ts, rows):
    n_ct = CONV_WIDTH // LANES

    @pl.when(pl.program_id(1) == 0)
    def _():
        hbuf[:, 0:CONV_HALO, :] = jnp.zeros((n_ct, CONV_HALO, LANES), F32)

    @pl.when(pl.program_id(1) != 0)
    def _():
        hbuf[:, 0:CONV_HALO, :] = hbuf[:, ts:ts + CONV_HALO, :]

    for c in range(n_ct):
        cs = slice(c * LANES, (c + 1) * LANES)
        a = a_ref[:, cs].astype(F32)
        gt = gate_ref[:, cs].astype(F32)
        hbuf[c, CONV_HALO:CONV_HALO + ts, :] = a * jax.nn.sigmoid(gt)

    base = CONV_HALO - (CONV_KERNEL - 1)
    for c in range(n_ct):
        cs = slice(c * LANES, (c + 1) * LANES)
        for r in range(ts // rows):
            acc = jnp.zeros((rows, LANES), F32) + b_ref[:, cs]
            for j in range(CONV_KERNEL):
                acc = acc + w_ref[j:j + 1, cs] * hbuf[c, pl.ds(r * rows + base + j, rows), :]
            ybuf[r * rows:(r + 1) * rows, cs] = acc

    y = ybuf[...]
    mu = jnp.mean(y, axis=-1, keepdims=True)
    yc = y - mu
    var = jnp.mean(yc * yc, axis=-1, keepdims=True)
    yn = yc * lax.rsqrt(var + LN_EPS) * lng_ref[...] + lnb_ref[...]
    o_ref[...] = (yn * jax.nn.sigmoid(yn)).astype(o_ref.dtype)


def _conv_branch(z, w, b, lng, lnb, l, *, batch, seq):
    t = z.shape[0]
    ts, rows = CONV_TS, CONV_ROWS
    nt = seq // ts
    n_ct = CONV_WIDTH // LANES
    vec = _layer_vec(l, CONV_WIDTH, 2)
    return pl.pallas_call(
        functools.partial(_conv_kernel, ts=ts, rows=rows),
        grid=(batch, nt),
        in_specs=[
            pl.BlockSpec((ts, CONV_WIDTH), lambda bi, si: (bi * nt + si, 0)),
            pl.BlockSpec((ts, CONV_WIDTH), lambda bi, si: (bi * nt + si, 1)),
            pl.BlockSpec((None, CONV_KERNEL, CONV_WIDTH), lambda bi, si: (l, 0, 0)),
            vec, vec, vec,
        ],
        out_specs=pl.BlockSpec((ts, CONV_WIDTH), lambda bi, si: (bi * nt + si, 0)),
        out_shape=jax.ShapeDtypeStruct((t, CONV_WIDTH), BF16),
        scratch_shapes=[
            pltpu.VMEM((n_ct, ts + CONV_HALO, LANES), F32),
            pltpu.VMEM((ts, CONV_WIDTH), F32),
        ],
        compiler_params=_params(2),
        name="conv_branch",
    )(z, z, w, b, lng, lnb)


def _attn_kernel(q_ref, k_ref, v_ref, lq1_ref, lk1_ref, lq2_ref, lk2_ref, sg_ref, o_ref,
                 *, seq, tq, lam_init):
    lam = (jnp.exp(jnp.sum(lq1_ref[...] * lk1_ref[...], axis=-1, keepdims=True))
           - jnp.exp(jnp.sum(lq2_ref[...] * lk2_ref[...], axis=-1, keepdims=True))
           + lam_init)

    lane = lax.broadcasted_iota(jnp.int32, (tq, V_HEAD_DIM), 1)
    row = lax.broadcasted_iota(jnp.int32, (tq, tq), 0)
    col = lax.broadcasted_iota(jnp.int32, (tq, tq), 1)
    causal = col <= row

    for qi in range(seq // tq):
        kend = (qi + 1) * tq
        q = q_ref[qi * tq:kend, :] * (DIFF_HEAD_DIM ** -0.5)
        zero = jnp.zeros_like(q)
        kb = k_ref[0:kend, :]
        probs = []
        for qm in (jnp.where(lane < DIFF_HEAD_DIM, q, zero), jnp.where(lane >= DIFF_HEAD_DIM, q, zero)):
            s = lax.dot_general(qm, kb, (((1,), (1,)), ((), ())), preferred_element_type=F32)
            diag = jnp.where(causal, s[:, kend - tq:], -jnp.inf)
            s = diag if qi == 0 else jnp.concatenate([s[:, :kend - tq], diag], axis=1)
            p = jnp.exp(s - jnp.max(s, axis=-1, keepdims=True))
            probs.append((p, 1.0 / jnp.sum(p, axis=-1, keepdims=True)))
        (p1, r1), (p2, r2) = probs
        a = p1 * r1 - p2 * (lam * r2)
        o = jnp.dot(a.astype(BF16), v_ref[0:kend, :], preferred_element_type=F32)
        o = _rms_rows(o, sg_ref[...]) * (1.0 - lam_init)
        o_ref[qi * tq:kend, :] = o.astype(o_ref.dtype)


def _diff_attention(z, lq1, lk1, lq2, lk2, sg, l, *, batch, seq, lam_init):
    t = z.shape[0]
    q0 = 2 * CONV_WIDTH // V_HEAD_DIM
    k0 = q0 + N_DIFF_HEADS
    v0 = k0 + N_DIFF_HEADS
    lam_spec = _layer_vec(l, DIFF_HEAD_DIM, 2)
    return pl.pallas_call(
        functools.partial(_attn_kernel, seq=seq, tq=ATTN_TQ, lam_init=lam_init),
        grid=(batch, N_DIFF_HEADS),
        in_specs=[
            pl.BlockSpec((seq, V_HEAD_DIM), lambda b, h: (b, q0 + h)),
            pl.BlockSpec((seq, V_HEAD_DIM), lambda b, h: (b, k0 + h)),
            pl.BlockSpec((seq, V_HEAD_DIM), lambda b, h: (b, v0 + h)),
            lam_spec, lam_spec, lam_spec, lam_spec,
            _layer_vec(l, V_HEAD_DIM, 2),
        ],
        out_specs=pl.BlockSpec((seq, V_HEAD_DIM), lambda b, h: (b, h)),
        out_shape=jax.ShapeDtypeStruct((t, ATTN_WIDTH), BF16),
        compiler_params=_params(2),
        name="diff_attention",
    )(z, z, z, lq1, lk1, lq2, lk2, sg)


def _out_proj_kernel(c_ref, a_ref, w_ref, x_ref, o_ref):
    acc = jnp.dot(c_ref[...], w_ref[0:CONV_WIDTH, :], preferred_element_type=F32)
    acc = acc + jnp.dot(a_ref[...], w_ref[CONV_WIDTH:, :], preferred_element_type=F32)
    o_ref[...] = x_ref[...] + acc


def _out_proj(conv_out, attn_out, w, x, l):
    t, d = x.shape
    tm = OUT_PROJ_TM
    return pl.pallas_call(
        _out_proj_kernel,
        grid=(t // tm,),
        in_specs=[
            pl.BlockSpec((tm, CONV_WIDTH), lambda i: (i, 0)),
            pl.BlockSpec((tm, ATTN_WIDTH), lambda i: (i, 0)),
            pl.BlockSpec((None, d, d), lambda i: (l, 0, 0)),
            pl.BlockSpec((tm, d), lambda i: (i, 0)),
        ],
        out_specs=pl.BlockSpec((tm, d), lambda i: (i, 0)),
        out_shape=jax.ShapeDtypeStruct((t, d), F32),
        compiler_params=_params(1),
        name="out_proj",
    )(conv_out, attn_out, w, x)


def _ffn_kernel(x_ref, g_ref, wg_ref, wu_ref, wd_ref, o_ref, h_ref):
    @pl.when(pl.program_id(1) == 0)
    def _():
        x = x_ref[...]
        h_ref[...] = _rms_rows(x, g_ref[...]).astype(BF16)
        o_ref[...] = x

    h = h_ref[...]
    gate = jnp.dot(h, wg_ref[...], preferred_element_type=F32)
    up = jnp.dot(h, wu_ref[...], preferred_element_type=F32)
    act = (gate * jax.nn.sigmoid(gate) * up).astype(BF16)
    o_ref[...] += jnp.dot(act, wd_ref[...], preferred_element_type=F32)


def _ffn(x, g, wg, wu, wd, l):
    t, d = x.shape
    f = wg.shape[2]
    tm, tf = FFN_TM, FFN_TF
    return pl.pallas_call(
        _ffn_kernel,
        grid=(t // tm, f // tf),
        in_specs=[
            pl.BlockSpec((tm, d), lambda i, j: (i, 0)),
            _layer_vec(l, d, 2),
            pl.BlockSpec((None, d, tf), lambda i, j: (l, 0, j)),
            pl.BlockSpec((None, d, tf), lambda i, j: (l, 0, j)),
            pl.BlockSpec((None, tf, d), lambda i, j: (l, j, 0)),
        ],
        out_specs=pl.BlockSpec((tm, d), lambda i, j: (i, 0)),
        out_shape=jax.ShapeDtypeStruct((t, d), F32),
        scratch_shapes=[pltpu.VMEM((tm, d), BF16)],
        compiler_params=_params(2),
        name="ffn",
    )(x, g, wg, wu, wd)


def _final_norm_kernel(x_ref, g_ref, o_ref):
    o_ref[...] = _rms_rows(x_ref[...], g_ref[...])


def _final_norm(x, g):
    t, d = x.shape
    tm = FINAL_TM
    return pl.pallas_call(
        _final_norm_kernel,
        grid=(t // tm,),
        in_specs=[pl.BlockSpec((tm, d), lambda i: (i, 0)), pl.BlockSpec((1, d), lambda i: (0, 0))],
        out_specs=pl.BlockSpec((tm, d), lambda i: (i, 0)),
        out_shape=jax.ShapeDtypeStruct((t, d), F32),
        compiler_params=_params(1),
        name="final_norm",
    )(x, g)


def kernel(x, norm1_g, w_in, conv_w, conv_b, conv_ln_g, conv_ln_b, lam_q1, lam_k1, lam_q2, lam_k2,
           subln_g, w_out, norm2_g, w_gate, w_up, w_down, final_g):
    batch, seq, d = x.shape
    depth = w_in.shape[0]
    xt = x.reshape(batch * seq, d)
    rows = lambda v: v.reshape(depth, 1, -1)
    norm1_g, norm2_g, conv_b, conv_ln_g, conv_ln_b, subln_g, lam_q1, lam_k1, lam_q2, lam_k2 = map(
        rows, (norm1_g, norm2_g, conv_b, conv_ln_g, conv_ln_b, subln_g, lam_q1, lam_k1, lam_q2, lam_k2))
    w_in, w_out, w_gate, w_up, w_down = (w.astype(BF16) for w in (w_in, w_out, w_gate, w_up, w_down))
    for l in range(depth):
        lam_init = 0.8 - 0.6 * math.exp(-0.3 * l)
        z = _norm_matmul(xt, norm1_g, w_in, l)
        conv_out = _conv_branch(z, conv_w, conv_b, conv_ln_g, conv_ln_b, l, batch=batch, seq=seq)
        attn_out = _diff_attention(z, lam_q1, lam_k1, lam_q2, lam_k2, subln_g, l,
                                   batch=batch, seq=seq, lam_init=lam_init)
        xt = _out_proj(conv_out, attn_out, w_out, xt, l)
        xt = _ffn(xt, norm2_g, w_gate, w_up, w_down, l)
    out = _final_norm(xt, final_g.reshape(1, d))
    return out.reshape(batch, seq, d)
```

```python
import functools
import math

import jax
import jax.numpy as jnp
from jax import lax
from jax.experimental import pallas as pl
from jax.experimental.pallas import tpu as pltpu

D_MODEL = 2048
CONV_WIDTH = 1024
ATTN_WIDTH = 1024
DIFF_HEAD_DIM = 64
V_HEAD_DIM = 128
N_DIFF_HEADS = 8
CONV_KERNEL = 31
EPS = 1e-6
LN_EPS = 1e-5

LANES = 128
CONV_HALO = 32
VMEM_LIMIT = 56 * 1024 * 1024

BF16 = jnp.bfloat16
F32 = jnp.float32

IN_PROJ_TM, IN_PROJ_TN = 1024, 1024
CONV_TS, CONV_ROWS = 256, 64
ATTN_TQ = 256
OUT_PROJ_TM = 512
FFN_TM, FFN_TF = 1024, 512

Q_PRESCALE = DIFF_HEAD_DIM ** -0.5 * math.log2(math.e)


def _rms_rows(x, g):
    ms = jnp.mean(x * x, axis=-1, keepdims=True)
    return x * lax.rsqrt(ms + EPS) * g


def _layer_vec(l, width, n_grid):
    if n_grid == 1:
        return pl.BlockSpec((None, 1, width), lambda i: (l, 0, 0))
    return pl.BlockSpec((None, 1, width), lambda i, j: (l, 0, 0))


def _params(n_grid):
    return pltpu.CompilerParams(
        dimension_semantics=("arbitrary",) * n_grid, vmem_limit_bytes=VMEM_LIMIT)


def _norm_matmul_kernel(x_ref, g_ref, w_ref, o_ref, h_ref, *, q_tile):
    j = pl.program_id(1)

    @pl.when(j == 0)
    def _():
        h_ref[...] = _rms_rows(x_ref[...], g_ref[...]).astype(BF16)

    acc = jnp.dot(h_ref[...], w_ref[...], preferred_element_type=F32)
    o_ref[...] = (acc * jnp.where(j == q_tile, Q_PRESCALE, 1.0)).astype(o_ref.dtype)


def _norm_matmul(x, g, w, l):
    t, d = x.shape
    n = w.shape[2]
    tm, tn = IN_PROJ_TM, IN_PROJ_TN
    assert tn == ATTN_WIDTH
    return pl.pallas_call(
        functools.partial(_norm_matmul_kernel, q_tile=2 * CONV_WIDTH // tn),
        grid=(t // tm, n // tn),
        in_specs=[
            pl.BlockSpec((tm, d), lambda i, j: (i, 0)),
            _layer_vec(l, d, 2),
            pl.BlockSpec((None, d, tn), lambda i, j: (l, 0, j)),
        ],
        out_specs=pl.BlockSpec((tm, tn), lambda i, j: (i, j)),
        out_shape=jax.ShapeDtypeStruct((t, n), BF16),
        scratch_shapes=[pltpu.VMEM((tm, d), BF16)],
        compiler_params=_params(2),
        name="norm_in_proj",
    )(x, g, w)


def _conv_kernel(a_ref, gate_ref, w_ref, b_ref, lng_ref, lnb_ref, o_ref, hbuf, ybuf, *, ts, rows):
    n_ct = CONV_WIDTH // LANES

    @pl.when(pl.program_id(1) == 0)
    def _():
        hbuf[:, 0:CONV_HALO, :] = jnp.zeros((n_ct, CONV_HALO, LANES), F32)

    @pl.when(pl.program_id(1) != 0)
    def _():
        hbuf[:, 0:CONV_HALO, :] = hbuf[:, ts:ts + CONV_HALO, :]

    for c in range(n_ct):
        cs = slice(c * LANES, (c + 1) * LANES)
        a = a_ref[:, cs].astype(F32)
        gt = gate_ref[:, cs].astype(F32)
        hbuf[c, CONV_HALO:CONV_HALO + ts, :] = a * jax.nn.sigmoid(gt)

    base = CONV_HALO - (CONV_KERNEL - 1)
    for c in range(n_ct):
        cs = slice(c * LANES, (c + 1) * LANES)
        for r in range(ts // rows):
            acc = jnp.zeros((rows, LANES), F32) + b_ref[:, cs]
            for j in range(CONV_KERNEL):
                acc = acc + w_ref[j:j + 1, cs] * hbuf[c, pl.ds(r * rows + base + j, rows), :]
            ybuf[r * rows:(r + 1) * rows, cs] = acc

    y = ybuf[...]
    mu = jnp.mean(y, axis=-1, keepdims=True)
    yc = y - mu
    var = jnp.mean(yc * yc, axis=-1, keepdims=True)
    yn = yc * lax.rsqrt(var + LN_EPS) * lng_ref[...] + lnb_ref[...]
    o_ref[...] = (yn * jax.nn.sigmoid(yn)).astype(o_ref.dtype)


def _conv_branch(z, w, b, lng, lnb, l, *, batch, seq):
    t = z.shape[0]
    ts, rows = CONV_TS, CONV_ROWS
    nt = seq // ts
    n_ct = CONV_WIDTH // LANES
    vec = _layer_vec(l, CONV_WIDTH, 2)
    return pl.pallas_call(
        functools.partial(_conv_kernel, ts=ts, rows=rows),
        grid=(batch, nt),
        in_specs=[
            pl.BlockSpec((ts, CONV_WIDTH), lambda bi, si: (bi * nt + si, 0)),
            pl.BlockSpec((ts, CONV_WIDTH), lambda bi, si: (bi * nt + si, 1)),
            pl.BlockSpec((None, CONV_KERNEL, CONV_WIDTH), lambda bi, si: (l, 0, 0)),
            vec, vec, vec,
        ],
        out_specs=pl.BlockSpec((ts, CONV_WIDTH), lambda bi, si: (bi * nt + si, 0)),
        out_shape=jax.ShapeDtypeStruct((t, CONV_WIDTH), BF16),
        scratch_shapes=[
            pltpu.VMEM((n_ct, ts + CONV_HALO, LANES), F32),
            pltpu.VMEM((ts, CONV_WIDTH), F32),
        ],
        compiler_params=_params(2),
        name="conv_branch",
    )(z, z, w, b, lng, lnb)


def _attn_kernel(q_ref, k_ref, v_ref, lq1_ref, lk1_ref, lq2_ref, lk2_ref, sg_ref, o_ref,
                 *, seq, tq, lam_init):
    lam = (jnp.exp(jnp.sum(lq1_ref[...] * lk1_ref[...], axis=-1, keepdims=True))
           - jnp.exp(jnp.sum(lq2_ref[...] * lk2_ref[...], axis=-1, keepdims=True))
           + lam_init)

    lane = lax.broadcasted_iota(jnp.int32, (tq, V_HEAD_DIM), 1)
    row = lax.broadcasted_iota(jnp.int32, (tq, tq), 0)
    col = lax.broadcasted_iota(jnp.int32, (tq, tq), 1)
    causal = col <= row

    for qi in range(seq // tq):
        kend = (qi + 1) * tq
        q = q_ref[qi * tq:kend, :]
        zero = jnp.zeros_like(q)
        kb = k_ref[0:kend, :]
        probs = []
        for qm in (jnp.where(lane < DIFF_HEAD_DIM, q, zero), jnp.where(lane >= DIFF_HEAD_DIM, q, zero)):
            s = lax.dot_general(qm, kb, (((1,), (1,)), ((), ())), preferred_element_type=F32)
            diag = jnp.where(causal, s[:, kend - tq:], -jnp.inf)
            s = diag if qi == 0 else jnp.concatenate([s[:, :kend - tq], diag], axis=1)
            p = jnp.exp2(s - jnp.max(s, axis=-1, keepdims=True))
            probs.append((p, 1.0 / jnp.sum(p, axis=-1, keepdims=True)))
        (p1, r1), (p2, r2) = probs
        a = p1 * r1 - p2 * (lam * r2)
        o = jnp.dot(a.astype(BF16), v_ref[0:kend, :], preferred_element_type=F32)
        o = _rms_rows(o, sg_ref[...]) * (1.0 - lam_init)
        o_ref[qi * tq:kend, :] = o.astype(o_ref.dtype)


def _diff_attention(z, lq1, lk1, lq2, lk2, sg, l, *, batch, seq, lam_init):
    t = z.shape[0]
    q0 = 2 * CONV_WIDTH // V_HEAD_DIM
    k0 = q0 + N_DIFF_HEADS
    v0 = k0 + N_DIFF_HEADS
    lam_spec = _layer_vec(l, DIFF_HEAD_DIM, 2)
    return pl.pallas_call(
        functools.partial(_attn_kernel, seq=seq, tq=ATTN_TQ, lam_init=lam_init),
        grid=(batch, N_DIFF_HEADS),
        in_specs=[
            pl.BlockSpec((seq, V_HEAD_DIM), lambda b, h: (b, q0 + h)),
            pl.BlockSpec((seq, V_HEAD_DIM), lambda b, h: (b, k0 + h)),
            pl.BlockSpec((seq, V_HEAD_DIM), lambda b, h: (b, v0 + h)),
            lam_spec, lam_spec, lam_spec, lam_spec,
            _layer_vec(l, V_HEAD_DIM, 2),
        ],
        out_specs=pl.BlockSpec((seq, V_HEAD_DIM), lambda b, h: (b, h)),
        out_shape=jax.ShapeDtypeStruct((t, ATTN_WIDTH), BF16),
        compiler_params=_params(2),
        name="diff_attention",
    )(z, z, z, lq1, lk1, lq2, lk2, sg)


def _out_proj_kernel(c_ref, a_ref, w_ref, x_ref, o_ref):
    acc = jnp.dot(c_ref[...], w_ref[0:CONV_WIDTH, :], preferred_element_type=F32)
    acc = acc + jnp.dot(a_ref[...], w_ref[CONV_WIDTH:, :], preferred_element_type=F32)
    o_ref[...] = x_ref[...] + acc


def _out_proj(conv_out, attn_out, w, x, l):
    t, d = x.shape
    tm = OUT_PROJ_TM
    return pl.pallas_call(
        _out_proj_kernel,
        grid=(t // tm,),
        in_specs=[
            pl.BlockSpec((tm, CONV_WIDTH), lambda i: (i, 0)),
            pl.BlockSpec((tm, ATTN_WIDTH), lambda i: (i, 0)),
            pl.BlockSpec((None, d, d), lambda i: (l, 0, 0)),
            pl.BlockSpec((tm, d), lambda i: (i, 0)),
        ],
        out_specs=pl.BlockSpec((tm, d), lambda i: (i, 0)),
        out_shape=jax.ShapeDtypeStruct((t, d), F32),
        compiler_params=_params(1),
        name="out_proj",
    )(conv_out, attn_out, w, x)


def _ffn_kernel(x_ref, g_ref, wg_ref, wu_ref, wd_ref, fg_ref, o_ref, h_ref, *, final_norm):
    j = pl.program_id(1)

    @pl.when(j == 0)
    def _():
        x = x_ref[...]
        h_ref[...] = _rms_rows(x, g_ref[...]).astype(BF16)
        o_ref[...] = x

    h = h_ref[...]
    gate = jnp.dot(h, wg_ref[...], preferred_element_type=F32)
    up = jnp.dot(h, wu_ref[...], preferred_element_type=F32)
    act = (gate * jax.nn.sigmoid(gate) * up).astype(BF16)
    o_ref[...] += jnp.dot(act, wd_ref[...], preferred_element_type=F32)

    if final_norm:
        @pl.when(j == pl.num_programs(1) - 1)
        def _():
            o_ref[...] = _rms_rows(o_ref[...], fg_ref[...])


def _ffn(x, g, wg, wu, wd, final_g, l, *, final_norm):
    t, d = x.shape
    f = wg.shape[2]
    tm, tf = FFN_TM, FFN_TF
    return pl.pallas_call(
        functools.partial(_ffn_kernel, final_norm=final_norm),
        grid=(t // tm, f // tf),
        in_specs=[
            pl.BlockSpec((tm, d), lambda i, j: (i, 0)),
            _layer_vec(l, d, 2),
            pl.BlockSpec((None, d, tf), lambda i, j: (l, 0, j)),
            pl.BlockSpec((None, d, tf), lambda i, j: (l, 0, j)),
            pl.BlockSpec((None, tf, d), lambda i, j: (l, j, 0)),
            pl.BlockSpec((1, d), lambda i, j: (0, 0)),
        ],
        out_specs=pl.BlockSpec((tm, d), lambda i, j: (i, 0)),
        out_shape=jax.ShapeDtypeStruct((t, d), F32),
        scratch_shapes=[pltpu.VMEM((tm, d), BF16)],
        compiler_params=_params(2),
        name="ffn",
    )(x, g, wg, wu, wd, final_g)


def kernel(x, norm1_g, w_in, conv_w, conv_b, conv_ln_g, conv_ln_b, lam_q1, lam_k1, lam_q2, lam_k2,
           subln_g, w_out, norm2_g, w_gate, w_up, w_down, final_g):
    batch, seq, d = x.shape
    depth = w_in.shape[0]
    xt = x.reshape(batch * seq, d)
    rows = lambda v: v.reshape(depth, 1, -1)
    norm1_g, norm2_g, conv_b, conv_ln_g, conv_ln_b, subln_g, lam_q1, lam_k1, lam_q2, lam_k2 = map(
        rows, (norm1_g, norm2_g, conv_b, conv_ln_g, conv_ln_b, subln_g, lam_q1, lam_k1, lam_q2, lam_k2))
    w_in, w_out, w_gate, w_up, w_down = (w.astype(BF16) for w in (w_in, w_out, w_gate, w_up, w_down))
    for l in range(depth):
        lam_init = 0.8 - 0.6 * math.exp(-0.3 * l)
        z = _norm_matmul(xt, norm1_g, w_in, l)
        conv_out = _conv_branch(z, conv_w, conv_b, conv_ln_g, conv_ln_b, l, batch=batch, seq=seq)
        attn_out = _diff_attention(z, lam_q1, lam_k1, lam_q2, lam_k2, subln_g, l,
                                   batch=batch, seq=seq, lam_init=lam_init)
        xt = _out_proj(conv_out, attn_out, w_out, xt, l)
        xt = _ffn(xt, norm2_g, w_gate, w_up, w_down, final_g.reshape(1, d), l,
                  final_norm=(l == depth - 1))
    return xt.reshape(batch, seq, d)
```
